```python
import math
import jax, jax.numpy as jnp
from jax import lax
import numpy as np

D_MODEL = 2048
BATCH = 2
SEQ = 8192
DEPTH = 1
DEC_BATCH = 1
DEC_SEQ = 16384
PAST_LEN = 128

MIX_WIDTH = D_MODEL
DIFF_WIDTH = MIX_WIDTH // 2
H_DIFF = 8
DH_DIFF = DIFF_WIDTH // (2 * H_DIFF)
DIL_WIDTH = MIX_WIDTH - DIFF_WIDTH
H_DIL = 8
DH_DIL = DIL_WIDTH // H_DIL
DIL_PATTERNS = ((128, 1), (512, 4), (2048, 16))
IN_WIDTH = 3 * DIFF_WIDTH + 3 * DIL_WIDTH
N_EXPERTS = 32
TOP_K = 4
D_FF = D_MODEL
SWIGLU_LIMIT = 7.0
SWIGLU_ALPHA = 1.702
Q_BLOCK = 128
MOE_BLOCK = 256
NORM_EPS = 1e-5
NEG_INF = -1e30

kernel_name = 'hybrid_diff_dilated_moe_encoder'


def rms_norm(x, g):
    xf = x.astype(jnp.float32)
    y = xf * lax.rsqrt(jnp.mean(xf * xf, axis=-1, keepdims=True) + NORM_EPS)
    return (y * g.astype(jnp.float32)).astype(x.dtype)


def alibi_slopes(n_heads):
    return jnp.asarray(2.0 ** (-8.0 * (np.arange(n_heads) + 1) / n_heads), dtype=jnp.float32)


def diff_attention(q, k, v, lam, slopes):
    b, s, h, _, dh = q.shape
    nq = s // Q_BLOCK
    scale = dh ** -0.5
    q_blocks = jnp.moveaxis(q.reshape(b, nq, Q_BLOCK, h, 2, dh), 1, 0)
    pos_k = jnp.arange(s)

    def one_block(args):
        qb, i = args
        logits = jnp.einsum('bqhcd,bkhcd->bchqk', qb, k, preferred_element_type=jnp.float32) * scale
        pos_q = i * Q_BLOCK + jnp.arange(Q_BLOCK)
        dist = jnp.abs(pos_q[:, None] - pos_k[None, :]).astype(jnp.float32)
        p = jax.nn.softmax(logits - slopes[:, None, None] * dist, axis=-1)
        a = (p[:, 0] - lam * p[:, 1]).astype(v.dtype)
        return jnp.einsum('bhqk,bkhe->bqhe', a, v)

    out = lax.map(one_block, (q_blocks, jnp.arange(nq)))
    return jnp.moveaxis(out, 0, 1).reshape(b, s, h, 2 * dh)


def dilated_branch(q, k, v, window, dil, slopes):
    b, s, h, dh = q.shape
    r = window // (2 * dil)
    unit = dil * r
    s_pad = -(-s // unit) * unit
    n = s_pad // dil
    nb = n // r

    def to_classes(t):
        t = jnp.pad(t, ((0, 0), (0, s_pad - s), (0, 0), (0, 0)))
        return t.reshape(b, n, dil, h, dh).transpose(0, 2, 1, 3, 4).reshape(b, dil, nb, r, h, dh)

    def neighbours(t):
        tp = jnp.pad(t, ((0, 0), (0, 0), (1, 1), (0, 0), (0, 0), (0, 0)))
        return jnp.concatenate([tp[:, :, :-2], tp[:, :, 1:-1], tp[:, :, 2:]], axis=3)

    qc = to_classes(q)
    kc = neighbours(to_classes(k))
    vc = neighbours(to_classes(v))
    off = jnp.arange(3 * r) - r
    rel = off[None, :] - jnp.arange(r)[:, None]
    jk = jnp.arange(nb)[:, None] * r + off[None, :]
    pos_k = jnp.arange(dil)[:, None, None] + dil * jk[None]
    valid = ((jk >= 0)[None] & (pos_k < s))[:, :, None, :] & (jnp.abs(rel) <= r)[None, None]
    logits = jnp.einsum('bgnqhd,bgnkhd->bgnhqk', qc, kc, preferred_element_type=jnp.float32) * (dh ** -0.5)
    bias = -slopes[:, None, None] * (dil * jnp.abs(rel)).astype(jnp.float32)
    logits = jnp.where(valid[None, :, :, None], logits + bias, NEG_INF)
    m = jnp.max(logits, axis=-1, keepdims=True)
    e = jnp.exp(logits - m)
    l = jnp.sum(e, axis=-1, keepdims=True)
    o = jnp.einsum('bgnhqk,bgnkhd->bgnqhd', (e / l).astype(v.dtype), vc)
    lse = jnp.swapaxes((m + jnp.log(l))[..., 0], 3, 4)

    def from_classes(t):
        t = t.reshape(b, dil, n, *t.shape[4:])
        t = jnp.moveaxis(t, 1, 2)
        return t.reshape(b, s_pad, *t.shape[3:])[:, :s]

    return from_classes(o), from_classes(lse)


def dilated_mixture(q, k, v, slopes):
    outs, lses = [], []
    for window, dil in DIL_PATTERNS:
        o, lse = dilated_branch(q, k, v, window, dil, slopes)
        outs.append(o)
        lses.append(lse)
    wts = jax.nn.softmax(jnp.stack(lses, 0), axis=0)
    return jnp.einsum('gbsh,gbshd->bshd', wts.astype(v.dtype), jnp.stack(outs, 0))


def moe_ffn(h, w_router, b_router, w_gate_up, b_gate_up, w_down, b_down):
    n, d = h.shape
    logits = jnp.dot(h, w_router, preferred_element_type=jnp.float32) + b_router.astype(jnp.float32)
    top_val, top_idx = lax.top_k(logits, TOP_K)
    gates = jax.nn.softmax(top_val, axis=-1)
    nk = n * TOP_K
    flat_e = top_idx.reshape(nk)
    order = jnp.argsort(flat_e, stable=True)
    sorted_e = flat_e[order]
    counts = jnp.bincount(flat_e, length=N_EXPERTS)
    padded = (counts + MOE_BLOCK - 1) // MOE_BLOCK * MOE_BLOCK
    seg_start = jnp.cumsum(counts) - counts
    pad_end = jnp.cumsum(padded)
    pad_start = pad_end - padded
    dest = pad_start[sorted_e] + jnp.arange(nk) - seg_start[sorted_e]
    n_blocks = -(-nk // MOE_BLOCK) + N_EXPERTS
    m_rows = n_blocks * MOE_BLOCK
    row_tok = jnp.full((m_rows,), n, jnp.int32).at[dest].set((order // TOP_K).astype(jnp.int32))
    row_gate = jnp.zeros((m_rows,), jnp.float32).at[dest].set(gates.reshape(nk)[order])
    block_expert = jnp.minimum(jnp.searchsorted(pad_end, jnp.arange(n_blocks) * MOE_BLOCK, side='right'), N_EXPERTS - 1)
    h_pad = jnp.concatenate([h, jnp.zeros((1, d), h.dtype)], axis=0)

    def expert_block(args):
        rows, e = args
        xb = h_pad[rows]
        gu = xb @ w_gate_up[e] + b_gate_up[e]
        gate = jnp.minimum(gu[:, 0::2], SWIGLU_LIMIT)
        up = jnp.clip(gu[:, 1::2], -SWIGLU_LIMIT, SWIGLU_LIMIT)
        act = (up + 1.0) * gate * jax.nn.sigmoid(SWIGLU_ALPHA * gate)
        return act @ w_down[e] + b_down[e]

    ys = lax.map(expert_block, (row_tok.reshape(n_blocks, MOE_BLOCK), block_expert))
    ys = ys.reshape(m_rows, d) * row_gate[:, None].astype(ys.dtype)
    return jnp.zeros((n + 1, d), ys.dtype).at[row_tok].add(ys)[:n].astype(h.dtype)


def encoder_layer(x, c, layer_idx, w_ada, b_ada, norm1_g, w_in, lam_q1, lam_k1, lam_q2, lam_k2,
                  subln_g, w_out, norm2_g, w_router, b_router, w_gate_up, b_gate_up, w_down, b_down):
    b, s, d = x.shape
    mod = jax.nn.silu(c) @ w_ada + b_ada
    shift1, scale1, gate1, shift2, scale2, gate2 = [t[:, None, :] for t in jnp.split(mod, 6, axis=-1)]

    hn = rms_norm(x, norm1_g) * (1.0 + scale1) + shift1
    proj = hn @ w_in
    cuts = [DIFF_WIDTH, 2 * DIFF_WIDTH, 3 * DIFF_WIDTH, 3 * DIFF_WIDTH + DIL_WIDTH, 3 * DIFF_WIDTH + 2 * DIL_WIDTH]
    qa, ka, va, qd, kd, vd = jnp.split(proj, cuts, axis=-1)
    qa = qa.reshape(b, s, H_DIFF, 2, DH_DIFF)
    ka = ka.reshape(b, s, H_DIFF, 2, DH_DIFF)
    va = va.reshape(b, s, H_DIFF, 2 * DH_DIFF)
    lam_init = 0.8 - 0.6 * math.exp(-0.3 * layer_idx)
    lam = (jnp.exp(jnp.sum(lam_q1.astype(jnp.float32) * lam_k1.astype(jnp.float32)))
           - jnp.exp(jnp.sum(lam_q2.astype(jnp.float32) * lam_k2.astype(jnp.float32))) + lam_init)
    oa = diff_attention(qa, ka, va, lam, alibi_slopes(H_DIFF))
    oa = (rms_norm(oa, subln_g) * (1.0 - lam_init)).reshape(b, s, DIFF_WIDTH)
    qd = qd.reshape(b, s, H_DIL, DH_DIL)
    kd = kd.reshape(b, s, H_DIL, DH_DIL)
    vd = vd.reshape(b, s, H_DIL, DH_DIL)
    od = dilated_mixture(qd, kd, vd, alibi_slopes(H_DIL)).reshape(b, s, DIL_WIDTH)
    mix = jnp.concatenate([oa.astype(x.dtype), od.astype(x.dtype)], axis=-1) @ w_out
    x = x + gate1 * mix

    hn2 = rms_norm(x, norm2_g) * (1.0 + scale2) + shift2
    ff = moe_ffn(hn2.reshape(b * s, d), w_router, b_router, w_gate_up, b_gate_up, w_down, b_down)
    return x + gate2 * ff.reshape(b, s, d)


def encoder(x, c, w_ada, b_ada, norm1_g, w_in, lam_q1, lam_k1, lam_q2, lam_k2, subln_g, w_out,
            norm2_g, w_router, b_router, w_gate_up, b_gate_up, w_down, b_down, final_g):
    for l in range(DEPTH):
        x = encoder_layer(x, c, l, w_ada[l], b_ada[l], norm1_g[l], w_in[l], lam_q1[l], lam_k1[l],
                          lam_q2[l], lam_k2[l], subln_g[l], w_out[l], norm2_g[l], w_router[l],
                          b_router[l], w_gate_up[l], b_gate_up[l], w_down[l], b_down[l])
    return rms_norm(x, final_g)


def setup_inputs(seed: int = 0) -> dict:
    key = jax.random.key(seed)
    ks = jax.random.split(key, 22)
    f32 = jnp.float32

    def nrm(k, shape, scale):
        return jax.random.normal(k, shape, f32) * scale

    return {
        'x_prompt': nrm(ks[0], (BATCH, SEQ, D_MODEL), 1.0),
        'x_sample': nrm(ks[1], (DEC_BATCH, DEC_SEQ, D_MODEL), 1.0),
        'c_prompt': nrm(ks[2], (BATCH, D_MODEL), 1.0),
        'c_sample': nrm(ks[3], (DEC_BATCH, D_MODEL), 1.0),
        'w_ada': nrm(ks[4], (DEPTH, D_MODEL, 6 * D_MODEL), D_MODEL ** -0.5),
        'b_ada': nrm(ks[5], (DEPTH, 6 * D_MODEL), 0.02),
        'norm1_g': 1.0 + nrm(ks[6], (DEPTH, D_MODEL), 0.02),
        'w_in': nrm(ks[7], (DEPTH, D_MODEL, IN_WIDTH), D_MODEL ** -0.5),
        'lam_q1': nrm(ks[8], (DEPTH, DH_DIFF), 0.1),
        'lam_k1': nrm(ks[9], (DEPTH, DH_DIFF), 0.1),
        'lam_q2': nrm(ks[10], (DEPTH, DH_DIFF), 0.1),
        'lam_k2': nrm(ks[11], (DEPTH, DH_DIFF), 0.1),
        'subln_g': 1.0 + nrm(ks[12], (DEPTH, 2 * DH_DIFF), 0.02),
        'w_out': nrm(ks[13], (DEPTH, MIX_WIDTH, D_MODEL), MIX_WIDTH ** -0.5),
        'norm2_g': 1.0 + nrm(ks[14], (DEPTH, D_MODEL), 0.02),
        'w_router': nrm(ks[15], (DEPTH, D_MODEL, N_EXPERTS), D_MODEL ** -0.5),
        'b_router': nrm(ks[16], (DEPTH, N_EXPERTS), 0.01),
        'w_gate_up': nrm(ks[17], (DEPTH, N_EXPERTS, D_MODEL, 2 * D_FF), D_MODEL ** -0.5),
        'b_gate_up': nrm(ks[18], (DEPTH, N_EXPERTS, 2 * D_FF), 0.02),
        'w_down': nrm(ks[19], (DEPTH, N_EXPERTS, D_FF, D_MODEL), D_FF ** -0.5),
        'b_down': nrm(ks[20], (DEPTH, N_EXPERTS, D_MODEL), 0.02),
        'final_g': 1.0 + nrm(ks[21], (D_MODEL,), 0.02),
    }


def reference(x_prompt, x_sample, c_prompt, c_sample, w_ada, b_ada, norm1_g, w_in, lam_q1, lam_k1,
              lam_q2, lam_k2, subln_g, w_out, norm2_g, w_router, b_router, w_gate_up, b_gate_up,
              w_down, b_down, final_g):
    y_prompt = encoder(x_prompt, c_prompt, w_ada, b_ada, norm1_g, w_in, lam_q1, lam_k1, lam_q2, lam_k2,
                       subln_g, w_out, norm2_g, w_router, b_router, w_gate_up, b_gate_up, w_down, b_down, final_g)
    y_sample = encoder(x_sample, c_sample, w_ada, b_ada, norm1_g, w_in, lam_q1, lam_k1, lam_q2, lam_k2,
                       subln_g, w_out, norm2_g, w_router, b_router, w_gate_up, b_gate_up, w_down, b_down, final_g)
    return (y_prompt, y_sample)
```

```python
import functools
import math

import jax
import jax.numpy as jnp
import numpy as np
from jax import lax
from jax.experimental import pallas as pl
from jax.experimental.pallas import tpu as pltpu

F32 = jnp.float32
BF16 = jnp.bfloat16

LANES = 128
NORM_EPS = 1e-5
NEG_BIG = -1e30

H_DIFF = 8
H_DIL = 8
DIL_PATTERNS = ((128, 1), (512, 4), (2048, 16))
N_EXPERTS = 32
TOP_K = 4
SWIGLU_LIMIT = 7.0
SWIGLU_ALPHA = 1.702

ATTN_TQ = 256
DIFF_TK = 512
DIL_TK = 256
ROW_TILE = 512
MOE_TM = 512
MOE_TF = 512
COMBINE_TT = 256
SCATTER_TT = 256

VMEM_LIMIT = 56 * 1024 * 1024


def _cparams(sem):
    return pltpu.CompilerParams(dimension_semantics=sem, vmem_limit_bytes=VMEM_LIMIT)


def _ada_kernel(c_ref, w_ref, b_ref, o_ref):
    c = c_ref[...]
    a = (c * jax.nn.sigmoid(c)).astype(BF16)
    o_ref[...] = jnp.dot(a, w_ref[...].astype(BF16), preferred_element_type=F32) + b_ref[...]


def _ada(c_pad, w_ada, b_ada):
    rows, d = c_pad.shape
    n_out = w_ada.shape[1]
    tn = 1024
    return pl.pallas_call(
        _ada_kernel,
        grid=(n_out // tn,),
        in_specs=[pl.BlockSpec((rows, d), lambda j: (0, 0)),
                  pl.BlockSpec((d, tn), lambda j: (0, j)),
                  pl.BlockSpec((1, tn), lambda j: (0, j))],
        out_specs=pl.BlockSpec((rows, tn), lambda j: (0, j)),
        out_shape=jax.ShapeDtypeStruct((rows, n_out), F32),
        compiler_params=_cparams(("arbitrary",)),
        name="ada",
    )(c_pad, w_ada, b_ada.reshape(1, n_out))


def _modulated_norm(x, g, scale, shift):
    ms = jnp.mean(x * x, axis=-1, keepdims=True)
    return (x * lax.rsqrt(ms + NORM_EPS) * g) * (1.0 + scale) + shift


def _inproj_kernel(x_ref, shift_ref, scale_ref, g_ref, w_ref, o_ref, *, tm, seg_rows, n_seg):
    i = pl.program_id(1)
    seg = jnp.minimum((i * tm) // seg_rows, n_seg - 1)
    hn = _modulated_norm(x_ref[...], g_ref[...], scale_ref[pl.ds(seg, 1), :], shift_ref[pl.ds(seg, 1), :])
    res = jnp.dot(hn.astype(BF16), w_ref[...], preferred_element_type=F32)
    for c in range(o_ref.shape[0]):
        o_ref[c] = res[:, c * LANES:(c + 1) * LANES].astype(BF16)


def _inproj(x, mod, g, w_bf16, *, seg_rows, n_seg):
    n, d = x.shape
    width = w_bf16.shape[1]
    tm, tn = ROW_TILE, 1536
    ncb = tn // LANES
    return pl.pallas_call(
        functools.partial(_inproj_kernel, tm=tm, seg_rows=seg_rows, n_seg=n_seg),
        grid=(width // tn, n // tm),
        in_specs=[pl.BlockSpec((tm, d), lambda j, i: (i, 0)),
                  pl.BlockSpec((mod.shape[0], d), lambda j, i: (0, 0)),
                  pl.BlockSpec((mod.shape[0], d), lambda j, i: (0, 1)),
                  pl.BlockSpec((1, d), lambda j, i: (0, 0)),
                  pl.BlockSpec((d, tn), lambda j, i: (0, j))],
        out_specs=pl.BlockSpec((ncb, tm, LANES), lambda j, i: (j, i, 0)),
        out_shape=jax.ShapeDtypeStruct((width // LANES, n, LANES), BF16),
        compiler_params=_cparams(("arbitrary", "arbitrary")),
        name="inproj",
    )(x, mod, mod, g.reshape(1, d), w_bf16)


def _online_softmax_step(s, vb, m_ref, l_ref, acc_ref, c):
    m_prev = m_ref[c]
    m_new = jnp.maximum(m_prev, jnp.max(s, axis=-1, keepdims=True))
    alpha = jnp.exp(m_prev - m_new)
    p = jnp.exp(s - m_new)
    l_ref[c] = alpha * l_ref[c] + jnp.sum(p, axis=-1, keepdims=True)
    acc_ref[c] = alpha * acc_ref[c] + jnp.dot(p.astype(BF16), vb, preferred_element_type=F32)
    m_ref[c] = m_new


def _nt_dot(a, b):
    return lax.dot_general(a, b, (((1,), (1,)), ((), ())), preferred_element_type=F32)


def _diff_attn_kernel(par_ref, q_ref, k_ref, v_ref, g_ref, o_ref, m_ref, l_ref, acc_ref,
                      *, tq, tk, seq, out_scale):
    h = pl.program_id(1)
    qi = pl.program_id(2)
    lam = par_ref[0]
    slope = par_ref[1 + h]
    dh = LANES // 2
    q = q_ref[0] * (dh ** -0.5)
    lane = lax.broadcasted_iota(jnp.int32, q.shape, 1)
    zero = jnp.zeros_like(q)
    qc = (jnp.where(lane < dh, q, zero), jnp.where(lane >= dh, q, zero))
    rel = (lax.broadcasted_iota(jnp.int32, (tq, tk), 1)
           - lax.broadcasted_iota(jnp.int32, (tq, tk), 0)).astype(F32) * slope
    m_ref[...] = jnp.full(m_ref.shape, NEG_BIG, F32)
    l_ref[...] = jnp.zeros(l_ref.shape, F32)
    acc_ref[...] = jnp.zeros(acc_ref.shape, F32)

    def body(j, carry):
        kb = k_ref[0, pl.ds(pl.multiple_of(j * tk, tk), tk), :]
        vb = v_ref[0, pl.ds(pl.multiple_of(j * tk, tk), tk), :]
        off = ((j * tk - qi * tq).astype(F32)) * slope
        bias = jnp.abs(rel + off)
        for c in range(2):
            s = _nt_dot(qc[c], kb) - bias
            _online_softmax_step(s, vb, m_ref, l_ref, acc_ref, c)
        return carry

    lax.fori_loop(0, seq // tk, body, 0)
    o = acc_ref[0] / l_ref[0] - lam * (acc_ref[1] / l_ref[1])
    ms = jnp.mean(o * o, axis=-1, keepdims=True)
    o_ref[...] = ((o * lax.rsqrt(ms + NORM_EPS) * g_ref[...]) * out_scale).astype(o_ref.dtype)


def _diff_attn(par, proj, g, *, seq, row0, n_seq, out_scale, q_col, k_col, v_col):
    tq, tk = ATTN_TQ, DIFF_TK
    nq = seq // tq
    qb0, sb0 = row0 // tq, row0 // seq
    return pl.pallas_call(
        functools.partial(_diff_attn_kernel, tq=tq, tk=tk, seq=seq, out_scale=out_scale),
        grid=(n_seq, H_DIFF, nq),
        in_specs=[pl.BlockSpec(memory_space=pltpu.SMEM),
                  pl.BlockSpec((1, tq, LANES), lambda b, h, i: (q_col + h, qb0 + b * nq + i, 0)),
                  pl.BlockSpec((1, seq, LANES), lambda b, h, i: (k_col + h, sb0 + b, 0)),
                  pl.BlockSpec((1, seq, LANES), lambda b, h, i: (v_col + h, sb0 + b, 0)),
                  pl.BlockSpec((1, LANES), lambda b, h, i: (0, 0))],
        out_specs=pl.BlockSpec((tq, LANES), lambda b, h, i: (b * nq + i, h)),
        out_shape=jax.ShapeDtypeStruct((n_seq * seq, H_DIFF * LANES), BF16),
        scratch_shapes=[pltpu.VMEM((2, tq, 1), F32), pltpu.VMEM((2, tq, 1), F32),
                        pltpu.VMEM((2, tq, LANES), F32)],
        compiler_params=_cparams(("arbitrary", "arbitrary", "arbitrary")),
        name="diff_attn",
    )(par, proj, proj, proj, g.reshape(1, LANES))


def _dil_bias_table(n_heads, tq, tk):
    reach = max(w // 2 for w, _ in DIL_PATTERNS)
    n_off = 2 * (reach // tk) + 1
    off = (np.arange(n_off) - n_off // 2) * tk
    delta = off[:, None, None] + np.arange(tk)[None, None, :] - np.arange(tq)[None, :, None]
    count = np.zeros(delta.shape, np.float64)
    for w, dil in DIL_PATTERNS:
        count += (delta % dil == 0) & (np.abs(delta) <= w // 2)
    slopes = 2.0 ** (-8.0 * (np.arange(n_heads) + 1) / n_heads)
    with np.errstate(divide="ignore"):
        logc = np.where(count > 0, np.log(np.maximum(count, 1.0)), NEG_BIG)
    tbl = -slopes[:, None, None, None] * np.abs(delta)[None] + logc[None]
    return jnp.asarray(np.maximum(tbl, NEG_BIG), F32), n_off


def _dil_attn_kernel(q_ref, k_ref, v_ref, tbl_ref, o_ref, m_ref, l_ref, acc_ref, *, tq, tk, seq, n_off, scale):
    qi = pl.program_id(2)
    q = q_ref[0]
    m_ref[...] = jnp.full(m_ref.shape, NEG_BIG, F32)
    l_ref[...] = jnp.zeros(l_ref.shape, F32)
    acc_ref[...] = jnp.zeros(acc_ref.shape, F32)
    half = n_off // 2
    j_lo = jnp.maximum(qi - half, 0)
    j_hi = jnp.minimum(qi + half + 1, seq // tk)

    def body(j, carry):
        kb = k_ref[0, pl.ds(pl.multiple_of(j * tk, tk), tk), :]
        vb = v_ref[0, pl.ds(pl.multiple_of(j * tk, tk), tk), :]
        s = _nt_dot(q, kb) * scale + tbl_ref[0, j - qi + half]
        _online_softmax_step(s, vb, m_ref, l_ref, acc_ref, 0)
        return carry

    lax.fori_loop(j_lo, j_hi, body, 0)
    o_ref[...] = (acc_ref[0] / l_ref[0]).astype(o_ref.dtype)


def _dil_attn(proj, tbl, n_off, *, seq, row0, n_seq, q_col, k_col, v_col):
    tq = tk = DIL_TK
    nq = seq // tq
    qb0, sb0 = row0 // tq, row0 // seq
    return pl.pallas_call(
        functools.partial(_dil_attn_kernel, tq=tq, tk=tk, seq=seq, n_off=n_off, scale=LANES ** -0.5),
        grid=(n_seq, H_DIL, nq),
        in_specs=[pl.BlockSpec((1, tq, LANES), lambda b, h, i: (q_col + h, qb0 + b * nq + i, 0)),
                  pl.BlockSpec((1, seq, LANES), lambda b, h, i: (k_col + h, sb0 + b, 0)),
                  pl.BlockSpec((1, seq, LANES), lambda b, h, i: (v_col + h, sb0 + b, 0)),
                  pl.BlockSpec((1, n_off, tq, tk), lambda b, h, i: (h, 0, 0, 0))],
        out_specs=pl.BlockSpec((tq, LANES), lambda b, h, i: (b * nq + i, h)),
        out_shape=jax.ShapeDtypeStruct((n_seq * seq, H_DIL * LANES), BF16),
        scratch_shapes=[pltpu.VMEM((1, tq, 1), F32), pltpu.VMEM((1, tq, 1), F32),
                        pltpu.VMEM((1, tq, LANES), F32)],
        compiler_params=_cparams(("arbitrary", "arbitrary", "arbitrary")),
        name="dil_attn",
    )(proj, proj, proj, tbl)


def _split_bf16(x):
    hi = x.astype(BF16)
    lo = (x - hi.astype(F32)).astype(BF16)
    return hi, lo


def _outproj_kernel(x_ref, oa0_ref, od0_ref, oa1_ref, od1_ref, wa_ref, wd_ref, gate1_ref, shift2_ref, scale2_ref,
                    g2_ref, wr_hi_ref, wr_lo_ref, br_ref, x1_ref, hn_ref, lg_ref, *, tm, seg_rows, n_seg, tiles0):
    i = pl.program_id(0)
    seg = jnp.minimum((i * tm) // seg_rows, n_seg - 1)
    first_group = i < tiles0
    oa = jnp.where(first_group, oa0_ref[...], oa1_ref[...])
    od = jnp.where(first_group, od0_ref[...], od1_ref[...])
    mix = (jnp.dot(oa, wa_ref[...], preferred_element_type=F32)
           + jnp.dot(od, wd_ref[...], preferred_element_type=F32))
    x1 = x_ref[...] + gate1_ref[pl.ds(seg, 1), :] * mix
    x1_ref[...] = x1
    hn = _modulated_norm(x1, g2_ref[...], scale2_ref[pl.ds(seg, 1), :], shift2_ref[pl.ds(seg, 1), :])
    hn_ref[...] = hn
    hi, lo = _split_bf16(hn)
    lg = (jnp.dot(hi, wr_hi_ref[...], preferred_element_type=F32)
          + jnp.dot(hi, wr_lo_ref[...], preferred_element_type=F32)
          + jnp.dot(lo, wr_hi_ref[...], preferred_element_type=F32))
    lg_ref[...] = lg + br_ref[...]


def _outproj(x, attn0, attn1, w_out_bf16, mod, g2, w_router, b_router, *, seg_rows, n_seg):
    n, d = x.shape
    (oa0, od0), (oa1, od1) = attn0, attn1
    half = oa0.shape[1]
    tm = ROW_TILE
    tiles0 = oa0.shape[0] // tm
    assert oa0.shape[0] % tm == 0 and oa0.shape[0] + oa1.shape[0] == n
    row0 = lambda i: (jnp.minimum(i, tiles0 - 1), 0)
    row1 = lambda i: (jnp.maximum(i - tiles0, 0), 0)
    ne = w_router.shape[1]
    wr = jnp.zeros((d, LANES), F32).at[:, :ne].set(w_router)
    wr_hi, wr_lo = _split_bf16(wr)
    br = jnp.full((1, LANES), NEG_BIG, F32).at[0, :ne].set(b_router)
    mrows = mod.shape[0]
    row = lambda i: (i, 0)
    fixed = lambda i: (0, 0)
    return pl.pallas_call(
        functools.partial(_outproj_kernel, tm=tm, seg_rows=seg_rows, n_seg=n_seg, tiles0=tiles0),
        grid=(n // tm,),
        in_specs=[pl.BlockSpec((tm, d), row),
                  pl.BlockSpec((tm, half), row0),
                  pl.BlockSpec((tm, half), row0),
                  pl.BlockSpec((tm, half), row1),
                  pl.BlockSpec((tm, half), row1),
                  pl.BlockSpec((half, d), lambda i: (0, 0)),
                  pl.BlockSpec((half, d), lambda i: (1, 0)),
                  pl.BlockSpec((mrows, d), lambda i: (0, 2)),
                  pl.BlockSpec((mrows, d), lambda i: (0, 3)),
                  pl.BlockSpec((mrows, d), lambda i: (0, 4)),
                  pl.BlockSpec((1, d), fixed),
                  pl.BlockSpec((d, LANES), fixed),
                  pl.BlockSpec((d, LANES), fixed),
                  pl.BlockSpec((1, LANES), fixed)],
        out_specs=[pl.BlockSpec((tm, d), row), pl.BlockSpec((tm, d), row), pl.BlockSpec((tm, LANES), row)],
        out_shape=[jax.ShapeDtypeStruct((n, d), F32), jax.ShapeDtypeStruct((n, d), F32),
                   jax.ShapeDtypeStruct((n, LANES), F32)],
        compiler_params=_cparams(("arbitrary",)),
        name="outproj",
    )(x, oa0, od0, oa1, od1, w_out_bf16, w_out_bf16, mod, mod, mod, g2.reshape(1, d), wr_hi, wr_lo, br)


def _deinterleave_kernel(w_ref, wg_ref, wu_ref):
    w = w_ref[0].astype(BF16)
    chunk = 2 * LANES
    r = lax.broadcasted_iota(jnp.int32, (chunk, LANES), 0)
    c = lax.broadcasted_iota(jnp.int32, (chunk, LANES), 1)
    sel_g = (r == 2 * c).astype(BF16)
    sel_u = (r == 2 * c + 1).astype(BF16)
    for j in range(w.shape[1] // chunk):
        wj = w[:, j * chunk:(j + 1) * chunk]
        wg_ref[0, :, j * LANES:(j + 1) * LANES] = jnp.dot(wj, sel_g, preferred_element_type=F32).astype(BF16)
        wu_ref[0, :, j * LANES:(j + 1) * LANES] = jnp.dot(wj, sel_u, preferred_element_type=F32).astype(BF16)


def _deinterleave_gate_up(w_gate_up):
    e, d, f2 = w_gate_up.shape
    tc = 1024
    out = jax.ShapeDtypeStruct((e, d, f2 // 2), BF16)
    return pl.pallas_call(
        _deinterleave_kernel,
        grid=(e, f2 // tc),
        in_specs=[pl.BlockSpec((1, d, tc), lambda i, j: (i, 0, j))],
        out_specs=[pl.BlockSpec((1, d, tc // 2), lambda i, j: (i, 0, j))] * 2,
        out_shape=[out, out],
        compiler_params=_cparams(("arbitrary", "arbitrary")),
        name="deinterleave_gate_up",
    )(w_gate_up)


def _cast_kernel(w_ref, o_ref):
    o_ref[...] = w_ref[...].astype(o_ref.dtype)


def _cast_bf16(w):
    e, a, b = w.shape
    return pl.pallas_call(
        _cast_kernel,
        grid=(e,),
        in_specs=[pl.BlockSpec((1, a, b), lambda i: (i, 0, 0))],
        out_specs=pl.BlockSpec((1, a, b), lambda i: (i, 0, 0)),
        out_shape=jax.ShapeDtypeStruct(w.shape, BF16),
        compiler_params=_cparams(("arbitrary",)),
        name="cast_bf16",
    )(w)


def _scatter_rows_kernel(pos_ref, h_ref, xs_ref, sem, *, tt):
    i = pl.program_id(0)

    def row_copy(t, k):
        dst = pos_ref[t * TOP_K + k]
        return pltpu.make_async_copy(h_ref.at[pl.ds(t, 1)], xs_ref.at[pl.ds(dst, 1)], sem)

    def issue(r, carry):
        for k in range(TOP_K):
            row_copy(i * tt + r, k).start()
        return carry

    def drain(r, carry):
        for k in range(TOP_K):
            row_copy(i * tt + r, k).wait()
        return carry

    lax.fori_loop(0, tt, issue, 0)
    lax.fori_loop(0, tt, drain, 0)


def _scatter_rows(pos_flat, h):
    n, d = h.shape
    tt = SCATTER_TT
    return pl.pallas_call(
        functools.partial(_scatter_rows_kernel, tt=tt),
        grid_spec=pltpu.PrefetchScalarGridSpec(
            num_scalar_prefetch=1,
            grid=(n // tt,),
            in_specs=[pl.BlockSpec(memory_space=pl.ANY)],
            out_specs=pl.BlockSpec(memory_space=pl.ANY),
            scratch_shapes=[pltpu.SemaphoreType.DMA(())]),
        out_shape=jax.ShapeDtypeStruct((n * TOP_K, d), h.dtype),
        compiler_params=_cparams(("arbitrary",)),
        name="scatter_rows",
    )(pos_flat, h)


def _experts_kernel(tile_ref, exp_ref, seg_ref, nsteps_ref, x_ref, wg_ref, wu_ref, bg_ref, bu_ref, wd_ref, bd_ref,
                    o_ref, acc_ref, *, tm, nf):
    s = pl.program_id(0)
    f = pl.program_id(1)

    @pl.when(s < nsteps_ref[0])
    def _():
        x = x_ref[...].astype(BF16)
        gate = jnp.dot(x, wg_ref[0], preferred_element_type=F32) + bg_ref[0]
        up = jnp.dot(x, wu_ref[0], preferred_element_type=F32) + bu_ref[0]
        gate = jnp.minimum(gate, SWIGLU_LIMIT)
        up = jnp.clip(up, -SWIGLU_LIMIT, SWIGLU_LIMIT)
        act = (up + 1.0) * gate * jax.nn.sigmoid(SWIGLU_ALPHA * gate)
        part = jnp.dot(act.astype(BF16), wd_ref[0], preferred_element_type=F32)

        @pl.when(f == 0)
        def _():
            acc_ref[...] = part

        @pl.when(f > 0)
        def _():
            acc_ref[...] += part

        @pl.when(f == nf - 1)
        def _():
            e = exp_ref[s]
            tile = tile_ref[s]
            rows = tile * tm + lax.broadcasted_iota(jnp.int32, (tm, 1), 0)
            mine = (rows >= seg_ref[e]) & (rows < seg_ref[e + 1])
            y = acc_ref[...] + bd_ref[0]
            first_visit = jnp.logical_or(s == 0, tile_ref[jnp.maximum(s - 1, 0)] != tile)

            @pl.when(first_visit)
            def _():
                o_ref[...] = jnp.where(mine, y, 0.0)

            @pl.when(jnp.logical_not(first_visit))
            def _():
                o_ref[...] = jnp.where(mine, y, o_ref[...])


def _experts(step_tile, step_exp, seg, n_steps, xs, wg, wu, bg, bu, wd, bd):
    m, d = xs.shape
    e, _, ff = wg.shape
    tm, tf = MOE_TM, MOE_TF
    nf = ff // tf
    n_steps_max = step_tile.shape[0]

    def chunk(s, f, ns):
        return jnp.where(s < ns[0], f, nf - 1)

    return pl.pallas_call(
        functools.partial(_experts_kernel, tm=tm, nf=nf),
        grid_spec=pltpu.PrefetchScalarGridSpec(
            num_scalar_prefetch=4,
            grid=(n_steps_max, nf),
            in_specs=[pl.BlockSpec((tm, d), lambda s, f, t, ex, sg, ns: (t[s], 0)),
                      pl.BlockSpec((1, d, tf), lambda s, f, t, ex, sg, ns: (ex[s], 0, chunk(s, f, ns))),
                      pl.BlockSpec((1, d, tf), lambda s, f, t, ex, sg, ns: (ex[s], 0, chunk(s, f, ns))),
                      pl.BlockSpec((1, 1, tf), lambda s, f, t, ex, sg, ns: (ex[s], 0, chunk(s, f, ns))),
                      pl.BlockSpec((1, 1, tf), lambda s, f, t, ex, sg, ns: (ex[s], 0, chunk(s, f, ns))),
                      pl.BlockSpec((1, tf, d), lambda s, f, t, ex, sg, ns: (ex[s], chunk(s, f, ns), 0)),
                      pl.BlockSpec((1, 1, d), lambda s, f, t, ex, sg, ns: (ex[s], 0, 0))],
            out_specs=pl.BlockSpec((tm, d), lambda s, f, t, ex, sg, ns: (t[s], 0)),
            scratch_shapes=[pltpu.VMEM((tm, d), F32)]),
        out_shape=jax.ShapeDtypeStruct((m, d), F32),
        compiler_params=_cparams(("arbitrary", "arbitrary")),
        name="experts",
    )(step_tile, step_exp, seg, n_steps, xs, wg, wu, bg, bu, wd, bd)


def _combine_kernel(pos_ref, ys_ref, x1_ref, gates_ref, gate2_ref, g_ref, o_ref, buf_ref, sem,
                    *, tt, seg_rows, n_seg, final_norm):
    i = pl.program_id(0)
    seg = jnp.minimum((i * tt) // seg_rows, n_seg - 1)

    def row_copy(r, k):
        src = pos_ref[(i * tt + r) * TOP_K + k]
        return pltpu.make_async_copy(ys_ref.at[pl.ds(src, 1)], buf_ref.at[k, pl.ds(r, 1)], sem)

    def issue(r, carry):
        for k in range(TOP_K):
            row_copy(r, k).start()
        return carry

    def drain(r, carry):
        for k in range(TOP_K):
            row_copy(r, k).wait()
        return carry

    lax.fori_loop(0, tt, issue, 0)
    lax.fori_loop(0, tt, drain, 0)
    gates = gates_ref[...]
    ff = gates[:, 0:1] * buf_ref[0]
    for k in range(1, TOP_K):
        ff = ff + gates[:, k:k + 1] * buf_ref[k]
    y = x1_ref[...] + gate2_ref[pl.ds(seg, 1), :] * ff
    if final_norm:
        ms = jnp.mean(y * y, axis=-1, keepdims=True)
        y = y * lax.rsqrt(ms + NORM_EPS) * g_ref[...]
    o_ref[...] = y


def _combine(pos_flat, ys, x1, gates, mod, final_g, *, seg_rows, n_seg, final_norm):
    n, d = x1.shape
    tt = COMBINE_TT
    mrows = mod.shape[0]
    return pl.pallas_call(
        functools.partial(_combine_kernel, tt=tt, seg_rows=seg_rows, n_seg=n_seg, final_norm=final_norm),
        grid_spec=pltpu.PrefetchScalarGridSpec(
            num_scalar_prefetch=1,
            grid=(n // tt,),
            in_specs=[pl.BlockSpec(memory_space=pl.ANY),
                      pl.BlockSpec((tt, d), lambda i, p: (i, 0)),
                      pl.BlockSpec((tt, TOP_K), lambda i, p: (i, 0)),
                      pl.BlockSpec((mrows, d), lambda i, p: (0, 5)),
                      pl.BlockSpec((1, d), lambda i, p: (0, 0))],
            out_specs=pl.BlockSpec((tt, d), lambda i, p: (i, 0)),
            scratch_shapes=[pltpu.VMEM((TOP_K, tt, d), F32), pltpu.SemaphoreType.DMA(())]),
        out_shape=jax.ShapeDtypeStruct((n, d), F32),
        compiler_params=_cparams(("arbitrary",)),
        name="combine",
    )(pos_flat, ys, x1, gates, mod, final_g.reshape(1, d))


def _route(logits, tm):
    n = logits.shape[0]
    top_val, top_idx = lax.top_k(logits, TOP_K)
    gates = jax.nn.softmax(top_val, axis=-1)
    multi_hot = jnp.sum(top_idx[:, :, None] == jnp.arange(N_EXPERTS)[None, None, :], axis=1).astype(jnp.int32)
    cum = jnp.cumsum(multi_hot, axis=0)
    counts = cum[-1]
    rank = jnp.take_along_axis(cum - multi_hot, top_idx, axis=1)
    seg_end = jnp.cumsum(counts)
    seg_start = seg_end - counts
    pos = (seg_start[top_idx] + rank).astype(jnp.int32)
    n_tiles = n * TOP_K // tm
    first_tile = seg_start // tm
    tiles_e = jnp.where(counts > 0, (seg_end - 1) // tm - first_tile + 1, 0)
    step_end = jnp.cumsum(tiles_e)
    n_steps = step_end[-1]
    n_steps_max = n_tiles + N_EXPERTS - 1
    steps = jnp.minimum(jnp.arange(n_steps_max), n_steps - 1)
    step_exp = jnp.minimum(jnp.searchsorted(step_end, steps, side="right"), N_EXPERTS - 1)
    step_tile = first_tile[step_exp] + steps - (step_end - tiles_e)[step_exp]
    seg = jnp.concatenate([seg_start, seg_end[-1:]]).astype(jnp.int32)
    return (pos.reshape(-1), gates, step_tile.astype(jnp.int32), step_exp.astype(jnp.int32), seg,
            n_steps.astype(jnp.int32).reshape(1))


def _alibi_slopes(n_heads):
    return 2.0 ** (-8.0 * (np.arange(n_heads) + 1) / n_heads)


def kernel(x_prompt, x_sample, c_prompt, c_sample, w_ada, b_ada, norm1_g, w_in, lam_q1, lam_k1, lam_q2, lam_k2,
           subln_g, w_out, norm2_g, w_router, b_router, w_gate_up, b_gate_up, w_down, b_down, final_g):
    bp, sp, d = x_prompt.shape
    bs, ss, _ = x_sample.shape
    depth = w_ada.shape[0]
    assert ss % sp == 0 and sp % ROW_TILE == 0
    n_seg = bp + bs
    x = jnp.concatenate([x_prompt.reshape(bp * sp, d), x_sample.reshape(bs * ss, d)], axis=0)
    n = x.shape[0]
    c = jnp.concatenate([c_prompt, c_sample], axis=0)
    c_pad = jnp.zeros((8, d), F32).at[:n_seg].set(c)
    seg_kw = dict(seg_rows=sp, n_seg=n_seg)
    groups = ((sp, 0, bp), (ss, bp * sp, bs))
    dil_tbl, n_off = _dil_bias_table(H_DIL, DIL_TK, DIL_TK)
    ncol = d // 2 // LANES

    for l in range(depth):
        mod = _ada(c_pad, w_ada[l], b_ada[l])
        proj = _inproj(x, mod, norm1_g[l], w_in[l].astype(BF16), **seg_kw)

        lam_init = 0.8 - 0.6 * math.exp(-0.3 * l)
        lam = (jnp.exp(jnp.sum(lam_q1[l] * lam_k1[l])) - jnp.exp(jnp.sum(lam_q2[l] * lam_k2[l])) + lam_init)
        par = jnp.concatenate([lam.reshape(1), jnp.asarray(_alibi_slopes(H_DIFF), F32)])
        attn = []
        for seq, row0, n_seq in groups:
            oa = _diff_attn(par, proj, subln_g[l], seq=seq, row0=row0, n_seq=n_seq,
                            out_scale=1.0 - lam_init, q_col=0, k_col=ncol, v_col=2 * ncol)
            od = _dil_attn(proj, dil_tbl, n_off, seq=seq, row0=row0, n_seq=n_seq,
                           q_col=3 * ncol, k_col=4 * ncol, v_col=5 * ncol)
            attn.append((oa, od))

        x1, hn2, logits = _outproj(x, attn[0], attn[1], w_out[l].astype(BF16), mod, norm2_g[l],
                                   w_router[l], b_router[l], **seg_kw)

        pos, gates, step_tile, step_exp, seg, n_steps = _route(logits[:, :N_EXPERTS], MOE_TM)
        wg, wu = _deinterleave_gate_up(w_gate_up[l])
        wd = _cast_bf16(w_down[l])
        bgu = b_gate_up[l].reshape(N_EXPERTS, -1, 2)
        bg = bgu[:, :, 0].reshape(N_EXPERTS, 1, -1)
        bu = bgu[:, :, 1].reshape(N_EXPERTS, 1, -1)
        xs = _scatter_rows(pos, hn2)
        ys = _experts(step_tile, step_exp, seg, n_steps, xs, wg, wu, bg, bu, wd,
                      b_down[l].reshape(N_EXPERTS, 1, d))
        x = _combine(pos, ys, x1, gates, mod, final_g, final_norm=(l == depth - 1), **seg_kw)

    return (x[:bp * sp].reshape(bp, sp, d), x[bp * sp:].reshape(bs, ss, d))
```

```python
import functools
import math

import jax
import jax.numpy as jnp
import numpy as np
from jax import lax
from jax.experimental import pallas as pl
from jax.experimental.pallas import tpu as pltpu

F32 = jnp.float32
BF16 = jnp.bfloat16

LANES = 128
NORM_EPS = 1e-5
NEG_BIG = -1e30

H_DIFF = 8
H_DIL = 8
DIL_PATTERNS = ((128, 1), (512, 4), (2048, 16))
N_EXPERTS = 32
TOP_K = 4
SWIGLU_LIMIT = 7.0
SWIGLU_ALPHA = 1.702

ATTN_TQ = 256
DIFF_TK = 512
DIL_TK = 256
ROW_TILE = 512
MOE_TM = 512
MOE_TF = 512
COMBINE_TT = 256
SCATTER_TT = 256

SKIP_LOG2 = 150.0
LOG2_E = math.log2(math.e)

VMEM_LIMIT = 56 * 1024 * 1024


def _cparams(sem):
    return pltpu.CompilerParams(dimension_semantics=sem, vmem_limit_bytes=VMEM_LIMIT)


def _ada_kernel(c_ref, w_ref, b_ref, o_ref):
    c = c_ref[...]
    a = (c * jax.nn.sigmoid(c)).astype(BF16)
    o_ref[...] = jnp.dot(a, w_ref[...].astype(BF16), preferred_element_type=F32) + b_ref[...]


def _ada(c_pad, w_ada, b_ada):
    rows, d = c_pad.shape
    n_out = w_ada.shape[1]
    tn = 1024
    return pl.pallas_call(
        _ada_kernel,
        grid=(n_out // tn,),
        in_specs=[pl.BlockSpec((rows, d), lambda j: (0, 0)),
                  pl.BlockSpec((d, tn), lambda j: (0, j)),
                  pl.BlockSpec((1, tn), lambda j: (0, j))],
        out_specs=pl.BlockSpec((rows, tn), lambda j: (0, j)),
        out_shape=jax.ShapeDtypeStruct((rows, n_out), F32),
        compiler_params=_cparams(("arbitrary",)),
        name="ada",
    )(c_pad, w_ada, b_ada.reshape(1, n_out))


def _modulated_norm(x, g, scale, shift):
    ms = jnp.mean(x * x, axis=-1, keepdims=True)
    return (x * lax.rsqrt(ms + NORM_EPS) * g) * (1.0 + scale) + shift


def _nt_dot(a, b):
    return lax.dot_general(a, b, (((1,), (1,)), ((), ())), preferred_element_type=F32)


def _inproj_kernel(x_ref, shift_ref, scale_ref, g_ref, w_ref, o_ref, *, tm, seg_rows, n_seg):
    i = pl.program_id(1)
    seg = jnp.minimum((i * tm) // seg_rows, n_seg - 1)
    hn = _modulated_norm(x_ref[...], g_ref[...], scale_ref[pl.ds(seg, 1), :], shift_ref[pl.ds(seg, 1), :])
    res = jnp.dot(hn.astype(BF16), w_ref[...], preferred_element_type=F32)
    for c in range(o_ref.shape[0]):
        o_ref[c] = res[:, c * LANES:(c + 1) * LANES].astype(BF16)


def _inproj(x, mod, g, w_bf16, *, seg_rows, n_seg):
    n, d = x.shape
    width = w_bf16.shape[1]
    tm, tn = ROW_TILE, 1024
    ncb = tn // LANES
    return pl.pallas_call(
        functools.partial(_inproj_kernel, tm=tm, seg_rows=seg_rows, n_seg=n_seg),
        grid=(width // tn, n // tm),
        in_specs=[pl.BlockSpec((tm, d), lambda j, i: (i, 0)),
                  pl.BlockSpec((mod.shape[0], d), lambda j, i: (0, 0)),
                  pl.BlockSpec((mod.shape[0], d), lambda j, i: (0, 1)),
                  pl.BlockSpec((1, d), lambda j, i: (0, 0)),
                  pl.BlockSpec((d, tn), lambda j, i: (0, j))],
        out_specs=pl.BlockSpec((ncb, tm, LANES), lambda j, i: (j, i, 0)),
        out_shape=jax.ShapeDtypeStruct((width // LANES, n, LANES), BF16),
        compiler_params=_cparams(("arbitrary", "arbitrary")),
        name="inproj",
    )(x, mod, mod, g.reshape(1, d), w_bf16)


def _inproj_t_kernel(x_ref, shift_ref, scale_ref, g_ref, wt_ref, oa_ref, od_ref, *, tm, seg_rows, n_seg):
    i = pl.program_id(0)
    seg = jnp.minimum((i * tm) // seg_rows, n_seg - 1)
    hn = _modulated_norm(x_ref[...], g_ref[...], scale_ref[pl.ds(seg, 1), :], shift_ref[pl.ds(seg, 1), :])
    res = _nt_dot(wt_ref[...], hn.astype(BF16))
    half = res.shape[0] // 2
    for o_ref, rows in ((oa_ref, res[:half]), (od_ref, res[half:])):
        tk = o_ref.shape[2]
        for c in range(tm // tk):
            o_ref[c] = rows[:, c * tk:(c + 1) * tk].astype(BF16)


def _inproj_t(x, mod, g, wt_bf16, *, seg_rows, n_seg):
    n, d = x.shape
    rows = wt_bf16.shape[0]
    tm = ROW_TILE
    half = rows // 2
    return pl.pallas_call(
        functools.partial(_inproj_t_kernel, tm=tm, seg_rows=seg_rows, n_seg=n_seg),
        grid=(n // tm,),
        in_specs=[pl.BlockSpec((tm, d), lambda i: (i, 0)),
                  pl.BlockSpec((mod.shape[0], d), lambda i: (0, 0)),
                  pl.BlockSpec((mod.shape[0], d), lambda i: (0, 1)),
                  pl.BlockSpec((1, d), lambda i: (0, 0)),
                  pl.BlockSpec((rows, d), lambda i: (0, 0))],
        out_specs=[pl.BlockSpec((tm // DIFF_TK, half, DIFF_TK), lambda i: (i, 0, 0)),
                   pl.BlockSpec((tm // DIL_TK, half, DIL_TK), lambda i: (i, 0, 0))],
        out_shape=[jax.ShapeDtypeStruct((n // DIFF_TK, half, DIFF_TK), BF16),
                   jax.ShapeDtypeStruct((n // DIL_TK, half, DIL_TK), BF16)],
        compiler_params=_cparams(("arbitrary",)),
        name="inproj_t",
    )(x, mod, mod, g.reshape(1, d), wt_bf16)


def _attn_t_kernel(jlo_ref, jhi_ref, par_ref, q_ref, k_ref, vt_ref, aux_ref, o_ref,
                   m_ref, l_ref, acc_ref, s_ref, p_ref, alpha_ref, *, tq, tk, nq, n_heads, diff, out_scale, half):
    b, h, qi = pl.program_id(0), pl.program_id(1), pl.program_id(2)
    idx = (b * n_heads + h) * nq + qi
    j_lo, j_hi = jlo_ref[idx], jhi_ref[idx]
    q = q_ref[0]
    if diff:
        dh = LANES // 2
        q = q * (dh ** -0.5)
        lane = lax.broadcasted_iota(jnp.int32, q.shape, 1)
        zero = jnp.zeros_like(q)
        qc = (jnp.where(lane < dh, q, zero), jnp.where(lane >= dh, q, zero))
        slope = par_ref[1 + h]
        rel = (lax.broadcasted_iota(jnp.int32, (tk, tq), 0)
               - lax.broadcasted_iota(jnp.int32, (tk, tq), 1)).astype(F32) * slope
    else:
        qc = (q,)
    ncomp = len(qc)
    m_ref[...] = jnp.full(m_ref.shape, NEG_BIG, F32)
    l_ref[...] = jnp.zeros(l_ref.shape, F32)
    acc_ref[...] = jnp.zeros(acc_ref.shape, F32)
    p_ref[...] = jnp.zeros(p_ref.shape, BF16)
    alpha_ref[...] = jnp.ones(alpha_ref.shape, F32)

    def scores(j):
        kb = k_ref[0, pl.ds(pl.multiple_of(j * tk, tk), tk), :]
        for c in range(ncomp):
            s_ref[c] = _nt_dot(kb, qc[c])

    def weighted_values(j):
        vt = vt_ref[j]
        for c in range(ncomp):
            acc_ref[c] = alpha_ref[c] * acc_ref[c] + jnp.dot(vt, p_ref[c], preferred_element_type=F32)

    def body(j, carry):
        weighted_values(jnp.maximum(j - 1, j_lo))
        if diff:
            off = ((j * tk - qi * tq).astype(F32)) * slope
            dist = jnp.abs(rel + off)
        else:
            bias = aux_ref[0, j - qi + half]
        for c in range(ncomp):
            s = s_ref[c] - dist if diff else s_ref[c] + bias
            m_prev = m_ref[c]
            m_new = jnp.maximum(m_prev, jnp.max(s, axis=0, keepdims=True))
            p = jnp.exp2(s - m_new)
            l_ref[c] = jnp.exp2(m_prev - m_new) * l_ref[c] + jnp.sum(p, axis=0, keepdims=True)
            alpha_ref[c] = jnp.exp2(m_prev - m_new)
            p_ref[c] = p.astype(BF16)
            m_ref[c] = m_new
        scores(jnp.minimum(j + 1, j_hi - 1))
        return carry

    scores(j_lo)
    lax.fori_loop(j_lo, j_hi, body, 0)
    weighted_values(j_hi - 1)
    if diff:
        lam = par_ref[0]
        o = acc_ref[0] / l_ref[0] - lam * (acc_ref[1] / l_ref[1])
        ms = jnp.mean(o * o, axis=0, keepdims=True)
        o = (o * lax.rsqrt(ms + NORM_EPS) * aux_ref[...]) * out_scale
    else:
        o = acc_ref[0] / l_ref[0]
    o_ref[...] = o.T.astype(o_ref.dtype)


def _attn_t(jlo, jhi, par, proj, vt, aux, *, diff, seq, row0, n_seq, tk, q_col, k_col, out_scale=1.0, half=0):
    tq = ATTN_TQ
    nq, nkv = seq // tq, seq // tk
    n_heads = H_DIFF if diff else H_DIL
    qb0, sb0 = row0 // tq, row0 // seq
    ncomp = 2 if diff else 1
    if diff:
        aux_spec = pl.BlockSpec((LANES, tq), lambda b, h, i, *_: (0, 0))
    else:
        aux_spec = pl.BlockSpec((1,) + aux.shape[1:], lambda b, h, i, *_: (h, 0, 0, 0))
    kern = functools.partial(_attn_t_kernel, tq=tq, tk=tk, nq=nq, n_heads=n_heads, diff=diff,
                             out_scale=out_scale, half=half)
    return pl.pallas_call(
        kern,
        grid_spec=pltpu.PrefetchScalarGridSpec(
            num_scalar_prefetch=3,
            grid=(n_seq, n_heads, nq),
            in_specs=[pl.BlockSpec((1, tq, LANES), lambda b, h, i, *_: (q_col + h, qb0 + b * nq + i, 0)),
                      pl.BlockSpec((1, seq, LANES), lambda b, h, i, *_: (k_col + h, sb0 + b, 0)),
                      pl.BlockSpec((nkv, LANES, tk), lambda b, h, i, *_: (sb0 + b, h, 0)),
                      aux_spec],
            out_specs=pl.BlockSpec((tq, LANES), lambda b, h, i, *_: (b * nq + i, h)),
            scratch_shapes=[pltpu.VMEM((ncomp, 1, tq), F32), pltpu.VMEM((ncomp, 1, tq), F32),
                            pltpu.VMEM((ncomp, LANES, tq), F32), pltpu.VMEM((ncomp, tk, tq), F32),
                            pltpu.VMEM((ncomp, tk, tq), BF16), pltpu.VMEM((ncomp, 1, tq), F32)]),
        out_shape=jax.ShapeDtypeStruct((n_seq * seq, n_heads * LANES), BF16),
        compiler_params=_cparams(("arbitrary", "arbitrary", "arbitrary")),
        name="diff_attn" if diff else "dil_attn",
    )(jlo, jhi, par, proj, proj, vt, aux)


def _dil_bias_table(n_heads, tq, tk):
    reach = max(w // 2 for w, _ in DIL_PATTERNS)
    n_off = 2 * (reach // tk) + 1
    off = (np.arange(n_off) - n_off // 2) * tk
    delta = off[:, None, None] + np.arange(tk)[None, :, None] - np.arange(tq)[None, None, :]
    count = np.zeros(delta.shape, np.float64)
    for w, dil in DIL_PATTERNS:
        count += (delta % dil == 0) & (np.abs(delta) <= w // 2)
    slopes = _alibi_slopes(n_heads) * LOG2_E
    logc = np.where(count > 0, np.log2(np.maximum(count, 1.0)), NEG_BIG)
    tbl = -slopes[:, None, None, None] * np.abs(delta)[None] + logc[None]
    return jnp.asarray(np.maximum(tbl, NEG_BIG), F32), n_off


def _row_norm_max(blocks, parts):
    hh, n, w = blocks.shape
    x = blocks.astype(F32)
    return jnp.sqrt(jnp.max(jnp.sum((x * x).reshape(hh, n, parts, w // parts), axis=-1), axis=-1))


def _key_tile_ranges(qn, kn, slopes, *, seq, row0, n_seq, tq, tk, scale, extra, band_half=None):
    hh = qn.shape[0]
    nq, nkv = seq // tq, seq // tk
    qmax = qn[:, row0:row0 + n_seq * seq].reshape(hh, n_seq, nq, tq).max(axis=-1)
    kmax = kn[:, row0:row0 + n_seq * seq].reshape(hh, n_seq, seq).max(axis=-1)
    bound = qmax * kmax[:, :, None] * (scale * 1.001)
    reach = jnp.minimum((SKIP_LOG2 + 2.0 * bound + extra) / slopes[:, None, None], float(seq))
    q0 = (jnp.arange(nq) * tq).astype(F32)
    j_lo = jnp.clip(jnp.floor((q0 - reach) / tk), 0, nkv - 1).astype(jnp.int32)
    j_hi = jnp.clip(jnp.floor((q0 + (tq - 1) + reach) / tk) + 1, 1, nkv).astype(jnp.int32)
    if band_half is not None:
        qi = jnp.arange(nq, dtype=jnp.int32)
        j_lo = jnp.maximum(j_lo, qi - band_half)
        j_hi = jnp.minimum(j_hi, qi + band_half + 1)
    order = lambda a: jnp.transpose(a, (1, 0, 2)).reshape(-1)
    return order(j_lo), order(j_hi)


def _split_bf16(x):
    hi = x.astype(BF16)
    lo = (x - hi.astype(F32)).astype(BF16)
    return hi, lo


def _outproj_kernel(x_ref, oa0_ref, od0_ref, oa1_ref, od1_ref, wa_ref, wd_ref, gate1_ref, shift2_ref, scale2_ref,
                    g2_ref, wr_hi_ref, wr_lo_ref, br_ref, x1_ref, hn_ref, lg_ref, *, tm, seg_rows, n_seg, tiles0):
    i = pl.program_id(0)
    seg = jnp.minimum((i * tm) // seg_rows, n_seg - 1)
    first_group = i < tiles0
    oa = jnp.where(first_group, oa0_ref[...], oa1_ref[...])
    od = jnp.where(first_group, od0_ref[...], od1_ref[...])
    mix = (jnp.dot(oa, wa_ref[...], preferred_element_type=F32)
           + jnp.dot(od, wd_ref[...], preferred_element_type=F32))
    x1 = x_ref[...] + gate1_ref[pl.ds(seg, 1), :] * mix
    x1_ref[...] = x1
    hn = _modulated_norm(x1, g2_ref[...], scale2_ref[pl.ds(seg, 1), :], shift2_ref[pl.ds(seg, 1), :])
    hn_ref[...] = hn
    hi, lo = _split_bf16(hn)
    lg = (jnp.dot(hi, wr_hi_ref[...], preferred_element_type=F32)
          + jnp.dot(hi, wr_lo_ref[...], preferred_element_type=F32)
          + jnp.dot(lo, wr_hi_ref[...], preferred_element_type=F32))
    lg_ref[...] = lg + br_ref[...]


def _outproj(x, attn0, attn1, w_out_bf16, mod, g2, w_router, b_router, *, seg_rows, n_seg):
    n, d = x.shape
    (oa0, od0), (oa1, od1) = attn0, attn1
    half = oa0.shape[1]
    tm = ROW_TILE
    tiles0 = oa0.shape[0] // tm
    assert oa0.shape[0] % tm == 0 and oa0.shape[0] + oa1.shape[0] == n
    row0 = lambda i: (jnp.minimum(i, tiles0 - 1), 0)
    row1 = lambda i: (jnp.maximum(i - tiles0, 0), 0)
    ne = w_router.shape[1]
    wr = jnp.zeros((d, LANES), F32).at[:, :ne].set(w_router)
    wr_hi, wr_lo = _split_bf16(wr)
    br = jnp.full((1, LANES), NEG_BIG, F32).at[0, :ne].set(b_router)
    mrows = mod.shape[0]
    row = lambda i: (i, 0)
    fixed = lambda i: (0, 0)
    return pl.pallas_call(
        functools.partial(_outproj_kernel, tm=tm, seg_rows=seg_rows, n_seg=n_seg, tiles0=tiles0),
        grid=(n // tm,),
        in_specs=[pl.BlockSpec((tm, d), row),
                  pl.BlockSpec((tm, half), row0),
                  pl.BlockSpec((tm, half), row0),
                  pl.BlockSpec((tm, half), row1),
                  pl.BlockSpec((tm, half), row1),
                  pl.BlockSpec((half, d), lambda i: (0, 0)),
                  pl.BlockSpec((half, d), lambda i: (1, 0)),
                  pl.BlockSpec((mrows, d), lambda i: (0, 2)),
                  pl.BlockSpec((mrows, d), lambda i: (0, 3)),
                  pl.BlockSpec((mrows, d), lambda i: (0, 4)),
                  pl.BlockSpec((1, d), fixed),
                  pl.BlockSpec((d, LANES), fixed),
                  pl.BlockSpec((d, LANES), fixed),
                  pl.BlockSpec((1, LANES), fixed)],
        out_specs=[pl.BlockSpec((tm, d), row), pl.BlockSpec((tm, d), row), pl.BlockSpec((tm, LANES), row)],
        out_shape=[jax.ShapeDtypeStruct((n, d), F32), jax.ShapeDtypeStruct((n, d), F32),
                   jax.ShapeDtypeStruct((n, LANES), F32)],
        compiler_params=_cparams(("arbitrary",)),
        name="outproj",
    )(x, oa0, od0, oa1, od1, w_out_bf16, w_out_bf16, mod, mod, mod, g2.reshape(1, d), wr_hi, wr_lo, br)


def _deinterleave_kernel(w_ref, wg_ref, wu_ref):
    w = w_ref[0].astype(BF16)
    chunk = 2 * LANES
    r = lax.broadcasted_iota(jnp.int32, (chunk, LANES), 0)
    c = lax.broadcasted_iota(jnp.int32, (chunk, LANES), 1)
    sel_g = (r == 2 * c).astype(BF16)
    sel_u = (r == 2 * c + 1).astype(BF16)
    for j in range(w.shape[1] // chunk):
        wj = w[:, j * chunk:(j + 1) * chunk]
        wg_ref[0, :, j * LANES:(j + 1) * LANES] = jnp.dot(wj, sel_g, preferred_element_type=F32).astype(BF16)
        wu_ref[0, :, j * LANES:(j + 1) * LANES] = jnp.dot(wj, sel_u, preferred_element_type=F32).astype(BF16)


def _deinterleave_gate_up(w_gate_up):
    e, d, f2 = w_gate_up.shape
    tc = 1024
    out = jax.ShapeDtypeStruct((e, d, f2 // 2), BF16)
    return pl.pallas_call(
        _deinterleave_kernel,
        grid=(e, f2 // tc),
        in_specs=[pl.BlockSpec((1, d, tc), lambda i, j: (i, 0, j))],
        out_specs=[pl.BlockSpec((1, d, tc // 2), lambda i, j: (i, 0, j))] * 2,
        out_shape=[out, out],
        compiler_params=_cparams(("arbitrary", "arbitrary")),
        name="deinterleave_gate_up",
    )(w_gate_up)


def _cast_kernel(w_ref, o_ref):
    o_ref[...] = w_ref[...].astype(o_ref.dtype)


def _cast_bf16(w):
    e, a, b = w.shape
    return pl.pallas_call(
        _cast_kernel,
        grid=(e,),
        in_specs=[pl.BlockSpec((1, a, b), lambda i: (i, 0, 0))],
        out_specs=pl.BlockSpec((1, a, b), lambda i: (i, 0, 0)),
        out_shape=jax.ShapeDtypeStruct(w.shape, BF16),
        compiler_params=_cparams(("arbitrary",)),
        name="cast_bf16",
    )(w)


def _scatter_rows_kernel(pos_ref, h_ref, xs_ref, sem, *, tt):
    i = pl.program_id(0)

    def row_copy(r, k):
        dst = pos_ref[(i * tt + r) * TOP_K + k]
        return pltpu.make_async_copy(h_ref.at[pl.ds(r, 1)], xs_ref.at[pl.ds(dst, 1)], sem)

    def issue(r, carry):
        for k in range(TOP_K):
            row_copy(r, k).start()
        return carry

    def drain(r, carry):
        for k in range(TOP_K):
            row_copy(r, k).wait()
        return carry

    lax.fori_loop(0, tt, issue, 0)
    lax.fori_loop(0, tt, drain, 0)


def _scatter_rows(pos_flat, h):
    n, d = h.shape
    tt = SCATTER_TT
    return pl.pallas_call(
        functools.partial(_scatter_rows_kernel, tt=tt),
        grid_spec=pltpu.PrefetchScalarGridSpec(
            num_scalar_prefetch=1,
            grid=(n // tt,),
            in_specs=[pl.BlockSpec((tt, d), lambda i, p: (i, 0))],
            out_specs=pl.BlockSpec(memory_space=pl.ANY),
            scratch_shapes=[pltpu.SemaphoreType.DMA(())]),
        out_shape=jax.ShapeDtypeStruct((n * TOP_K, d), h.dtype),
        compiler_params=_cparams(("arbitrary",)),
        name="scatter_rows",
    )(pos_flat, h)


def _experts_kernel(tile_ref, exp_ref, seg_ref, nsteps_ref, x_ref, wg_ref, wu_ref, bg_ref, bu_ref, wd_ref, bd_ref,
                    o_ref, acc_ref, *, tm, nf):
    s = pl.program_id(0)
    f = pl.program_id(1)

    @pl.when(s < nsteps_ref[0])
    def _():
        x = x_ref[...].astype(BF16)
        gate = jnp.dot(x, wg_ref[0], preferred_element_type=F32) + bg_ref[0]
        up = jnp.dot(x, wu_ref[0], preferred_element_type=F32) + bu_ref[0]
        gate = jnp.minimum(gate, SWIGLU_LIMIT)
        up = jnp.clip(up, -SWIGLU_LIMIT, SWIGLU_LIMIT)
        act = (up + 1.0) * gate * jax.nn.sigmoid(SWIGLU_ALPHA * gate)
        part = jnp.dot(act.astype(BF16), wd_ref[0], preferred_element_type=F32)

        @pl.when(f == 0)
        def _():
            acc_ref[...] = part

        @pl.when(f > 0)
        def _():
            acc_ref[...] += part

        @pl.when(f == nf - 1)
        def _():
            e = exp_ref[s]
            tile = tile_ref[s]
            rows = tile * tm + lax.broadcasted_iota(jnp.int32, (tm, 1), 0)
            mine = (rows >= seg_ref[e]) & (rows < seg_ref[e + 1])
            y = acc_ref[...] + bd_ref[0]
            first_visit = jnp.logical_or(s == 0, tile_ref[jnp.maximum(s - 1, 0)] != tile)

            @pl.when(first_visit)
            def _():
                o_ref[...] = jnp.where(mine, y, 0.0)

            @pl.when(jnp.logical_not(first_visit))
            def _():
                o_ref[...] = jnp.where(mine, y, o_ref[...])


def _experts(step_tile, step_exp, seg, n_steps, xs, wg, wu, bg, bu, wd, bd):
    m, d = xs.shape
    e, _, ff = wg.shape
    tm, tf = MOE_TM, MOE_TF
    nf = ff // tf
    n_steps_max = step_tile.shape[0]

    def chunk(s, f, ns):
        return jnp.where(s < ns[0], f, nf - 1)

    return pl.pallas_call(
        functools.partial(_experts_kernel, tm=tm, nf=nf),
        grid_spec=pltpu.PrefetchScalarGridSpec(
            num_scalar_prefetch=4,
            grid=(n_steps_max, nf),
            in_specs=[pl.BlockSpec((tm, d), lambda s, f, t, ex, sg, ns: (t[s], 0)),
                      pl.BlockSpec((1, d, tf), lambda s, f, t, ex, sg, ns: (ex[s], 0, chunk(s, f, ns))),
                      pl.BlockSpec((1, d, tf), lambda s, f, t, ex, sg, ns: (ex[s], 0, chunk(s, f, ns))),
                      pl.BlockSpec((1, 1, tf), lambda s, f, t, ex, sg, ns: (ex[s], 0, chunk(s, f, ns))),
                      pl.BlockSpec((1, 1, tf), lambda s, f, t, ex, sg, ns: (ex[s], 0, chunk(s, f, ns))),
                      pl.BlockSpec((1, tf, d), lambda s, f, t, ex, sg, ns: (ex[s], chunk(s, f, ns), 0)),
                      pl.BlockSpec((1, 1, d), lambda s, f, t, ex, sg, ns: (ex[s], 0, 0))],
            out_specs=pl.BlockSpec((tm, d), lambda s, f, t, ex, sg, ns: (t[s], 0)),
            scratch_shapes=[pltpu.VMEM((tm, d), F32)]),
        out_shape=jax.ShapeDtypeStruct((m, d), F32),
        compiler_params=_cparams(("arbitrary", "arbitrary")),
        name="experts",
    )(step_tile, step_exp, seg, n_steps, xs, wg, wu, bg, bu, wd, bd)


def _combine_kernel(pos_ref, ys_ref, x1_ref, gates_ref, gate2_ref, g_ref, o_ref, buf_ref, sem,
                    *, tt, seg_rows, n_seg, final_norm):
    i = pl.program_id(0)
    seg = jnp.minimum((i * tt) // seg_rows, n_seg - 1)

    def row_copy(r, k):
        src = pos_ref[(i * tt + r) * TOP_K + k]
        return pltpu.make_async_copy(ys_ref.at[pl.ds(src, 1)], buf_ref.at[k, pl.ds(r, 1)], sem)

    def issue(r, carry):
        for k in range(TOP_K):
            row_copy(r, k).start()
        return carry

    def drain(r, carry):
        for k in range(TOP_K):
            row_copy(r, k).wait()
        return carry

    lax.fori_loop(0, tt, issue, 0)
    lax.fori_loop(0, tt, drain, 0)
    gates = gates_ref[...]
    ff = gates[:, 0:1] * buf_ref[0]
    for k in range(1, TOP_K):
        ff = ff + gates[:, k:k + 1] * buf_ref[k]
    y = x1_ref[...] + gate2_ref[pl.ds(seg, 1), :] * ff
    if final_norm:
        ms = jnp.mean(y * y, axis=-1, keepdims=True)
        y = y * lax.rsqrt(ms + NORM_EPS) * g_ref[...]
    o_ref[...] = y


def _combine(pos_flat, ys, x1, gates, mod, final_g, *, seg_rows, n_seg, final_norm):
    n, d = x1.shape
    tt = COMBINE_TT
    mrows = mod.shape[0]
    return pl.pallas_call(
        functools.partial(_combine_kernel, tt=tt, seg_rows=seg_rows, n_seg=n_seg, final_norm=final_norm),
        grid_spec=pltpu.PrefetchScalarGridSpec(
            num_scalar_prefetch=1,
            grid=(n // tt,),
            in_specs=[pl.BlockSpec(memory_space=pl.ANY),
                      pl.BlockSpec((tt, d), lambda i, p: (i, 0)),
                      pl.BlockSpec((tt, TOP_K), lambda i, p: (i, 0)),
                      pl.BlockSpec((mrows, d), lambda i, p: (0, 5)),
                      pl.BlockSpec((1, d), lambda i, p: (0, 0))],
            out_specs=pl.BlockSpec((tt, d), lambda i, p: (i, 0)),
            scratch_shapes=[pltpu.VMEM((TOP_K, tt, d), F32), pltpu.SemaphoreType.DMA(())]),
        out_shape=jax.ShapeDtypeStruct((n, d), F32),
        compiler_params=_cparams(("arbitrary",)),
        name="combine",
    )(pos_flat, ys, x1, gates, mod, final_g.reshape(1, d))


def _route(logits, tm):
    n = logits.shape[0]
    top_val, top_idx = lax.top_k(logits, TOP_K)
    gates = jax.nn.softmax(top_val, axis=-1)
    multi_hot = jnp.sum(top_idx[:, :, None] == jnp.arange(N_EXPERTS)[None, None, :], axis=1).astype(jnp.int32)
    cum = jnp.cumsum(multi_hot, axis=0)
    counts = cum[-1]
    rank = jnp.take_along_axis(cum - multi_hot, top_idx, axis=1)
    seg_end = jnp.cumsum(counts)
    seg_start = seg_end - counts
    pos = (seg_start[top_idx] + rank).astype(jnp.int32)
    n_tiles = n * TOP_K // tm
    first_tile = seg_start // tm
    tiles_e = jnp.where(counts > 0, (seg_end - 1) // tm - first_tile + 1, 0)
    step_end = jnp.cumsum(tiles_e)
    n_steps = step_end[-1]
    n_steps_max = n_tiles + N_EXPERTS - 1
    steps = jnp.minimum(jnp.arange(n_steps_max), n_steps - 1)
    step_exp = jnp.minimum(jnp.searchsorted(step_end, steps, side="right"), N_EXPERTS - 1)
    step_tile = first_tile[step_exp] + steps - (step_end - tiles_e)[step_exp]
    seg = jnp.concatenate([seg_start, seg_end[-1:]]).astype(jnp.int32)
    return (pos.reshape(-1), gates, step_tile.astype(jnp.int32), step_exp.astype(jnp.int32), seg,
            n_steps.astype(jnp.int32).reshape(1))


def _alibi_slopes(n_heads):
    return 2.0 ** (-8.0 * (np.arange(n_heads) + 1) / n_heads)


def kernel(x_prompt, x_sample, c_prompt, c_sample, w_ada, b_ada, norm1_g, w_in, lam_q1, lam_k1, lam_q2, lam_k2,
           subln_g, w_out, norm2_g, w_router, b_router, w_gate_up, b_gate_up, w_down, b_down, final_g):
    bp, sp, d = x_prompt.shape
    bs, ss, _ = x_sample.shape
    depth = w_ada.shape[0]
    assert ss % sp == 0 and sp % ROW_TILE == 0
    n_seg = bp + bs
    x = jnp.concatenate([x_prompt.reshape(bp * sp, d), x_sample.reshape(bs * ss, d)], axis=0)
    c = jnp.concatenate([c_prompt, c_sample], axis=0)
    c_pad = jnp.zeros((8, d), F32).at[:n_seg].set(c)
    seg_kw = dict(seg_rows=sp, n_seg=n_seg)
    groups = ((sp, 0, bp), (ss, bp * sp, bs))
    dil_tbl, n_off = _dil_bias_table(H_DIL, ATTN_TQ, DIL_TK)
    wd_ = d // 2
    ncol = wd_ // LANES
    slopes_diff = jnp.asarray(_alibi_slopes(H_DIFF) * LOG2_E, F32)
    slopes_dil = jnp.asarray(_alibi_slopes(H_DIL) * LOG2_E, F32)
    dil_scale = LANES ** -0.5

    for l in range(depth):
        mod = _ada(c_pad, w_ada[l], b_ada[l])
        wl = w_in[l]
        w_qk = jnp.concatenate([wl[:, :wd_] * LOG2_E, wl[:, wd_:2 * wd_],
                                wl[:, 3 * wd_:4 * wd_] * (dil_scale * LOG2_E), wl[:, 4 * wd_:5 * wd_]],
                               axis=1).astype(BF16)
        w_vt = jnp.concatenate([wl[:, 2 * wd_:3 * wd_], wl[:, 5 * wd_:]], axis=1).T.astype(BF16)
        proj = _inproj(x, mod, norm1_g[l], w_qk, **seg_kw)
        vat, vdt = _inproj_t(x, mod, norm1_g[l], w_vt, **seg_kw)

        lam_init = 0.8 - 0.6 * math.exp(-0.3 * l)
        lam = (jnp.exp(jnp.sum(lam_q1[l] * lam_k1[l])) - jnp.exp(jnp.sum(lam_q2[l] * lam_k2[l])) + lam_init)
        par = jnp.concatenate([lam.reshape(1), slopes_diff])
        gain = jnp.broadcast_to(subln_g[l].reshape(LANES, 1), (LANES, ATTN_TQ))
        qn_a, kn_a = _row_norm_max(proj[:ncol], 2), _row_norm_max(proj[ncol:2 * ncol], 2)
        qn_d, kn_d = _row_norm_max(proj[2 * ncol:3 * ncol], 1), _row_norm_max(proj[3 * ncol:], 1)
        attn = []
        for seq, row0, n_seq in groups:
            geo = dict(seq=seq, row0=row0, n_seq=n_seq)
            jlo, jhi = _key_tile_ranges(qn_a, kn_a, slopes_diff, tq=ATTN_TQ, tk=DIFF_TK,
                                        scale=(LANES // 2) ** -0.5, extra=0.0, **geo)
            oa = _attn_t(jlo, jhi, par, proj, vat, gain, diff=True, tk=DIFF_TK, q_col=0, k_col=ncol,
                         out_scale=1.0 - lam_init, **geo)
            jlo, jhi = _key_tile_ranges(qn_d, kn_d, slopes_dil, tq=ATTN_TQ, tk=DIL_TK, scale=1.0,
                                        extra=math.log2(len(DIL_PATTERNS)), band_half=n_off // 2, **geo)
            od = _attn_t(jlo, jhi, par, proj, vdt, dil_tbl, diff=False, tk=DIL_TK, q_col=2 * ncol,
                         k_col=3 * ncol, half=n_off // 2, **geo)
            attn.append((oa, od))

        x1, hn2, logits = _outproj(x, attn[0], attn[1], w_out[l].astype(BF16), mod, norm2_g[l],
                                   w_router[l], b_router[l], **seg_kw)

        pos, gates, step_tile, step_exp, seg, n_steps = _route(logits[:, :N_EXPERTS], MOE_TM)
        wg, wu = _deinterleave_gate_up(w_gate_up[l])
        wd = _cast_bf16(w_down[l])
        bgu = b_gate_up[l].reshape(N_EXPERTS, -1, 2)
        bg = bgu[:, :, 0].reshape(N_EXPERTS, 1, -1)
        bu = bgu[:, :, 1].reshape(N_EXPERTS, 1, -1)
        xs = _scatter_rows(pos, hn2)
        ys = _experts(step_tile, step_exp, seg, n_steps, xs, wg, wu, bg, bu, wd,
                      b_down[l].reshape(N_EXPERTS, 1, d))
        x = _combine(pos, ys, x1, gates, mod, final_g, final_norm=(l == depth - 1), **seg_kw)

    return (x[:bp * sp].reshape(bp, sp, d), x[bp * sp:].reshape(bs, ss, d))
```
